```python
import math
import jax, jax.numpy as jnp
from jax import lax
import numpy as np

D_MODEL = 1024
BATCH = 32
SEQ = 256
DEPTH = 2
DEC_BATCH = 8
DEC_SEQ = 1024
PAST_LEN = 256

GRID_W = 64
HEAD_DIM = 64
N_Q_HEADS = 8
N_KV_HEADS = 2
Q_PER_KV = N_Q_HEADS // N_KV_HEADS
ATTN_WIDTH = N_Q_HEADS * HEAD_DIM
KV_WIDTH = N_KV_HEADS * HEAD_DIM
WINDOW = 128
BLOCK = 128
ROPE_BASE = 10000.0
ROT_AXIS = HEAD_DIM // 2
SG_GROUPS = 8
SG_CHUNK = 128
SG_WIDTH = D_MODEL - ATTN_WIDTH
SG_GROUP_DIM = SG_WIDTH // SG_GROUPS
IN_WIDTH = ATTN_WIDTH + 2 * KV_WIDTH + 2 * SG_WIDTH
SPLITS = (ATTN_WIDTH, ATTN_WIDTH + KV_WIDTH, ATTN_WIDTH + 2 * KV_WIDTH,
          ATTN_WIDTH + 2 * KV_WIDTH + SG_WIDTH)
PEER_HEADS = 8
PEER_KEYS = 128
PEER_EXPERTS = PEER_KEYS * PEER_KEYS
PEER_QDIM = 256
PEER_HALF = PEER_QDIM // 2
PEER_TOPK = 16
TOKEN_BLOCK = 128
LN_EPS = 1e-5
NEG_INF = -1e30
DEEPNORM_ALPHA = (2.0 * DEPTH) ** 0.25
DEEPNORM_BETA = (8.0 * DEPTH) ** -0.25

kernel_name = 'hymba_gmlp_swa_peer_deepnorm_step'


def layer_norm(x, g, b):
    xf = x.astype(jnp.float32)
    mu = jnp.mean(xf, axis=-1, keepdims=True)
    var = jnp.mean(jnp.square(xf - mu), axis=-1, keepdims=True)
    return ((xf - mu) * lax.rsqrt(var + LN_EPS) * g.astype(jnp.float32)
            + b.astype(jnp.float32)).astype(x.dtype)


def modulation(cond, w_mod, b_mod):
    m = jax.nn.silu(cond) @ w_mod + b_mod
    return jnp.split(m[..., None, :], 6, axis=-1)


def axial_rope(n_tokens):
    rows = n_tokens // GRID_W
    row = jnp.repeat(jnp.arange(rows), GRID_W).astype(jnp.float32)
    col = jnp.tile(jnp.arange(GRID_W), rows).astype(jnp.float32)
    inv_freq = 1.0 / (ROPE_BASE ** (jnp.arange(0, ROT_AXIS, 2, dtype=jnp.float32) / ROT_AXIS))
    ang_r = row[:, None] * inv_freq[None, :]
    ang_c = col[:, None] * inv_freq[None, :]
    return (jnp.cos(ang_r), jnp.sin(ang_r), jnp.cos(ang_c), jnp.sin(ang_c))


def _rotate(xh, cos, sin):
    x1, x2 = jnp.split(xh, 2, axis=-1)
    cos = cos[:, None, :]
    sin = sin[:, None, :]
    return jnp.concatenate([x1 * cos - x2 * sin, x1 * sin + x2 * cos], axis=-1)


def apply_axial_rope(x, rope):
    cr, sr, cc, sc = rope
    xf = x.astype(jnp.float32)
    out = jnp.concatenate([_rotate(xf[..., :ROT_AXIS], cr, sr),
                           _rotate(xf[..., ROT_AXIS:], cc, sc)], axis=-1)
    return out.astype(x.dtype)


def project_heads(h, w_in):
    B, T = h.shape[0], h.shape[1]
    p = h @ w_in
    q, k, v, su, sv = jnp.split(p, SPLITS, axis=-1)
    q = q.reshape(B, T, N_Q_HEADS, HEAD_DIM)
    k = k.reshape(B, T, N_KV_HEADS, HEAD_DIM)
    v = v.reshape(B, T, N_KV_HEADS, HEAD_DIM)
    return q, k, v, jax.nn.gelu(su), jax.nn.gelu(sv)


def sink_probs(logits, sink):
    s = jnp.broadcast_to(sink.astype(jnp.float32)[:, :, None, None], logits.shape[:-1] + (1,))
    p = jax.nn.softmax(jnp.concatenate([s, logits], axis=-1), axis=-1)
    return p[..., 1:]


def context_attention(q, k, v, sink):
    B, S = q.shape[0], q.shape[1]
    nb = S // BLOCK
    scale = HEAD_DIM ** -0.5
    qb = q.reshape(B, nb, BLOCK, N_KV_HEADS, Q_PER_KV, HEAD_DIM).transpose(1, 0, 2, 3, 4, 5)

    def one_block(qi):
        logits = jnp.einsum('bqkgd,bskd->bkgqs', qi, k).astype(jnp.float32) * scale
        p = sink_probs(logits, sink)
        return jnp.einsum('bkgqs,bskd->bqkgd', p.astype(v.dtype), v)

    o = lax.map(one_block, qb)
    return o.transpose(1, 0, 2, 3, 4, 5).reshape(B, S, ATTN_WIDTH)


def latent_attention(q, k, v, k_ctx, v_ctx, sink):
    B, T = q.shape[0], q.shape[1]
    nb = T // BLOCK
    scale = HEAD_DIM ** -0.5
    qb = q.reshape(B, nb, BLOCK, N_KV_HEADS, Q_PER_KV, HEAD_DIM)

    def band(x):
        xp = jnp.pad(x, ((0, 0), (BLOCK, BLOCK), (0, 0), (0, 0)))
        xp = xp.reshape(B, nb + 2, BLOCK, N_KV_HEADS, HEAD_DIM)
        return jnp.concatenate([xp[:, :-2], xp[:, 1:-1], xp[:, 2:]], axis=2)

    kb, vb = band(k), band(v)
    qpos = jnp.arange(nb)[:, None] * BLOCK + jnp.arange(BLOCK)[None, :]
    kpos = jnp.arange(nb)[:, None] * BLOCK - BLOCK + jnp.arange(3 * BLOCK)[None, :]
    mask = ((jnp.abs(qpos[:, :, None] - kpos[:, None, :]) <= WINDOW)
            & (kpos[:, None, :] >= 0) & (kpos[:, None, :] < T))
    lb = jnp.einsum('bnqkgd,bnjkd->bnkgqj', qb, kb).astype(jnp.float32) * scale
    lb = jnp.where(mask[None, :, None, None], lb, NEG_INF)
    lc = jnp.einsum('bnqkgd,blkd->bnkgql', qb, k_ctx).astype(jnp.float32) * scale
    p = sink_probs(jnp.concatenate([lb, lc], axis=-1), sink)
    pb = p[..., :3 * BLOCK].astype(v.dtype)
    pc = p[..., 3 * BLOCK:].astype(v.dtype)
    o = (jnp.einsum('bnkgqj,bnjkd->bnqkgd', pb, vb)
         + jnp.einsum('bnkgql,blkd->bnqkgd', pc, v_ctx))
    return o.reshape(B, T, ATTN_WIDTH)


def spatial_gating(u, v, ln_g, ln_b, w_s, b_s):
    B, T = u.shape[0], u.shape[1]
    v = layer_norm(v, ln_g, ln_b)
    vc = v.reshape(B, T // SG_CHUNK, SG_CHUNK, SG_GROUPS, SG_GROUP_DIM)
    s = jnp.einsum('gpq,bnqgc->bnpgc', w_s, vc) + b_s.T[:, :, None]
    return u * s.reshape(B, T, SG_WIDTH)


def peer(h, w_q, key1, key2, u_tab, v_tab):
    B, T, D = h.shape
    n = B * T
    x = h.reshape(n, D)
    q = (x @ w_q).reshape(n, PEER_HEADS, 2, PEER_HALF).astype(jnp.float32)
    s1 = jnp.einsum('nhd,kd->nhk', q[:, :, 0], key1.astype(jnp.float32))
    s2 = jnp.einsum('nhd,kd->nhk', q[:, :, 1], key2.astype(jnp.float32))
    t1, i1 = lax.top_k(s1, PEER_TOPK)
    t2, i2 = lax.top_k(s2, PEER_TOPK)
    cand = (t1[..., :, None] + t2[..., None, :]).reshape(n, PEER_HEADS, PEER_TOPK * PEER_TOPK)
    cand_idx = (i1[..., :, None] * PEER_KEYS + i2[..., None, :]).reshape(n, PEER_HEADS, PEER_TOPK * PEER_TOPK)
    top, pos = lax.top_k(cand, PEER_TOPK)
    idx = jnp.take_along_axis(cand_idx, pos, axis=-1)
    g = jax.nn.softmax(top, axis=-1)
    nblk = n // TOKEN_BLOCK

    def experts(args):
        xb, ib, gb = args
        a = jnp.einsum('thkd,td->thk', u_tab[ib], xb)
        w = (jax.nn.gelu(a.astype(jnp.float32)) * gb).astype(xb.dtype)
        return jnp.einsum('thk,thkd->td', w, v_tab[ib])

    out = lax.map(experts, (x.reshape(nblk, TOKEN_BLOCK, D),
                            idx.reshape(nblk, TOKEN_BLOCK, PEER_HEADS, PEER_TOPK),
                            g.reshape(nblk, TOKEN_BLOCK, PEER_HEADS, PEER_TOPK)))
    return out.reshape(B, T, D)


def post_norm_residual(x, y, gate, g, b):
    return layer_norm(DEEPNORM_ALPHA * x + gate * y, g, b)


def trunk_layer(x, mod, lw, rope, ctx_kv):
    (w_in, sg_ln_g, sg_ln_b, w_s, b_s, sink, w_o, ln1_g, ln1_b,
     peer_wq, peer_k1, peer_k2, peer_u, peer_v, ln2_g, ln2_b) = lw
    shift1, scale1, gate1, shift2, scale2, gate2 = mod
    h = x * (1.0 + scale1) + shift1
    q, k, v, su, sv = project_heads(h, w_in)
    sink = sink.reshape(N_KV_HEADS, Q_PER_KV)
    if ctx_kv is None:
        attn = context_attention(q, k, v, sink)
    else:
        q = apply_axial_rope(q, rope)
        k = apply_axial_rope(k, rope)
        attn = latent_attention(q, k, v, ctx_kv[0], ctx_kv[1], sink)
    sg = spatial_gating(su, sv, sg_ln_g, sg_ln_b, w_s, b_s)
    y = jnp.concatenate([attn, sg], axis=-1) @ w_o
    x = post_norm_residual(x, y, gate1, ln1_g, ln1_b)
    h = x * (1.0 + scale2) + shift2
    x = post_norm_residual(x, peer(h, peer_wq, peer_k1, peer_k2, peer_u, peer_v), gate2, ln2_g, ln2_b)
    return x, k, v


def setup_inputs(seed: int = 0) -> dict:
    key = jax.random.key(seed)
    ks = jax.random.split(key, 32)
    f32 = jnp.float32
    nrm = lambda k, shape, s: (jax.random.normal(k, shape, f32) * s)
    D = D_MODEL
    return {
        'x_prompt': nrm(ks[0], (BATCH, SEQ, D), 1.0),
        'x_sample': nrm(ks[1], (DEC_BATCH, DEC_SEQ, D), 1.0),
        'cache_k': nrm(ks[2], (DEC_BATCH, DEPTH, PAST_LEN, N_KV_HEADS, HEAD_DIM), 1.0),
        'cache_v': nrm(ks[3], (DEC_BATCH, DEPTH, PAST_LEN, N_KV_HEADS, HEAD_DIM), 1.0),
        'c': nrm(ks[4], (DEC_BATCH, D), 1.0),
        'c_ctx': nrm(ks[5], (D,), 1.0),
        'w_mod': nrm(ks[6], (DEPTH, D, 6 * D), D ** -0.5),
        'b_mod': nrm(ks[7], (DEPTH, 6 * D), 0.02),
        'w_in': nrm(ks[8], (DEPTH, D, IN_WIDTH), D ** -0.5),
        'sg_ln_g': 1.0 + nrm(ks[9], (DEPTH, SG_WIDTH), 0.02),
        'sg_ln_b': nrm(ks[10], (DEPTH, SG_WIDTH), 0.02),
        'w_s': nrm(ks[11], (DEPTH, SG_GROUPS, SG_CHUNK, SG_CHUNK), SG_CHUNK ** -0.5),
        'b_s': nrm(ks[12], (DEPTH, SG_GROUPS, SG_CHUNK), 0.02),
        'attn_sink': nrm(ks[13], (DEPTH, N_Q_HEADS), 0.1),
        'w_o': nrm(ks[14], (DEPTH, D, D), D ** -0.5 * DEEPNORM_BETA),
        'ln1_g': 1.0 + nrm(ks[15], (DEPTH, D), 0.02),
        'ln1_b': nrm(ks[16], (DEPTH, D), 0.02),
        'peer_wq': nrm(ks[17], (DEPTH, D, PEER_HEADS * PEER_QDIM), D ** -0.5),
        'peer_k1': nrm(ks[18], (DEPTH, PEER_KEYS, PEER_HALF), PEER_HALF ** -0.5),
        'peer_k2': nrm(ks[19], (DEPTH, PEER_KEYS, PEER_HALF), PEER_HALF ** -0.5),
        'peer_u': nrm(ks[20], (DEPTH, PEER_EXPERTS, D), D ** -0.5),
        'peer_v': nrm(ks[21], (DEPTH, PEER_EXPERTS, D), DEEPNORM_BETA),
        'ln2_g': 1.0 + nrm(ks[22], (DEPTH, D), 0.02),
        'ln2_b': nrm(ks[23], (DEPTH, D), 0.02),
    }


def reference(x_prompt, x_sample, cache_k, cache_v, c, c_ctx, w_mod, b_mod, w_in, sg_ln_g, sg_ln_b,
              w_s, b_s, attn_sink, w_o, ln1_g, ln1_b, peer_wq, peer_k1, peer_k2, peer_u, peer_v,
              ln2_g, ln2_b):
    rope = axial_rope(x_sample.shape[1])
    xp = x_prompt
    xs = x_sample
    new_k = []
    new_v = []
    for l in range(DEPTH):
        lw = (w_in[l], sg_ln_g[l], sg_ln_b[l], w_s[l], b_s[l], attn_sink[l], w_o[l], ln1_g[l], ln1_b[l],
              peer_wq[l], peer_k1[l], peer_k2[l], peer_u[l], peer_v[l], ln2_g[l], ln2_b[l])
        mod_ctx = modulation(c_ctx, w_mod[l], b_mod[l])
        mod_lat = modulation(c, w_mod[l], b_mod[l])
        xp, k_l, v_l = trunk_layer(xp, mod_ctx, lw, None, None)
        new_k.append(k_l)
        new_v.append(v_l)
        xs, _, _ = trunk_layer(xs, mod_lat, lw, rope, (cache_k[:, l], cache_v[:, l]))
    state_k = jnp.stack(new_k, axis=1)
    state_v = jnp.stack(new_v, axis=1)
    return (xp, xs, state_k, state_v)
```

```python
import functools
import math

import jax
import jax.numpy as jnp
from jax import lax
from jax.experimental import pallas as pl
from jax.experimental.pallas import tpu as pltpu

F32 = jnp.float32
BF16 = jnp.bfloat16

GRID_W = 64
HEAD_DIM = 64
N_Q_HEADS = 8
N_KV_HEADS = 2
Q_PER_KV = N_Q_HEADS // N_KV_HEADS
ATTN_WIDTH = N_Q_HEADS * HEAD_DIM
KV_WIDTH = N_KV_HEADS * HEAD_DIM
WINDOW = 128
BLOCK = 128
ROPE_BASE = 10000.0
ROT_AXIS = HEAD_DIM // 2
ROT_HALF = ROT_AXIS // 2
SG_GROUPS = 8
SG_CHUNK = 128
PEER_HEADS = 8
PEER_KEYS = 128
PEER_QDIM = 256
PEER_HALF = PEER_QDIM // 2
PEER_TOPK = 16
LN_EPS = 1e-5
MOD_ROWS = 16
NEG = -1e30

VMEM_LIMIT = 56 * 1024 * 1024


def _cparams(sem):
    return pltpu.CompilerParams(dimension_semantics=sem, vmem_limit_bytes=VMEM_LIMIT)


def _gelu(x):
    return jax.nn.gelu(x, approximate=True)


def _layer_norm(x, g, b):
    mu = jnp.mean(x, axis=-1, keepdims=True)
    xc = x - mu
    var = jnp.mean(xc * xc, axis=-1, keepdims=True)
    return xc * lax.rsqrt(var + LN_EPS) * g + b


def _mod_kernel(cond_ref, w_ref, b_ref, o_ref):
    c = cond_ref[...]
    a = c * jax.nn.sigmoid(c)
    o_ref[0] = jnp.dot(a, w_ref[0], preferred_element_type=F32,
                       precision=lax.Precision.HIGHEST) + b_ref[0]


def _modulation(cond, w_mod, b_mod):
    depth, d, n = w_mod.shape
    tn = 1536
    return pl.pallas_call(
        _mod_kernel,
        grid=(depth, n // tn),
        in_specs=[pl.BlockSpec((MOD_ROWS, d), lambda l, j: (0, 0)),
                  pl.BlockSpec((1, d, tn), lambda l, j: (l, 0, j)),
                  pl.BlockSpec((1, 1, tn), lambda l, j: (l, 0, j))],
        out_specs=pl.BlockSpec((1, MOD_ROWS, tn), lambda l, j: (l, 0, j)),
        out_shape=jax.ShapeDtypeStruct((depth, MOD_ROWS, n), F32),
        compiler_params=_cparams(("arbitrary", "arbitrary")),
        name="modulation",
    )(cond, w_mod, b_mod.reshape(depth, 1, n))


def _swap_halves(x, width):
    lane = lax.broadcasted_iota(jnp.int32, x.shape, 1)
    first = (lane % (2 * ROT_HALF)) < ROT_HALF
    return jnp.where(first, pltpu.roll(x, width - ROT_HALF, 1), pltpu.roll(x, ROT_HALF, 1))


def _inproj_kernel(x_ref, mod_ref, w_ref, lng_ref, lnb_ref, cos_ref, sin_ref,
                   q_ref, k_ref, v_ref, kr_ref, su_ref, sv_ref):
    d = x_ref.shape[1]
    m = mod_ref[0]
    shift, scale = m[:, 0:d], m[:, d:2 * d]
    h = (x_ref[...] * (1.0 + scale) + shift).astype(BF16)
    p = jnp.dot(h, w_ref[...], preferred_element_type=F32)
    q = p[:, :ATTN_WIDTH]
    k = p[:, ATTN_WIDTH:ATTN_WIDTH + KV_WIDTH]
    v = p[:, ATTN_WIDTH + KV_WIDTH:ATTN_WIDTH + 2 * KV_WIDTH]
    o = ATTN_WIDTH + 2 * KV_WIDTH
    sgw = (p.shape[1] - o) // 2
    su = _gelu(p[:, o:o + sgw])
    sv = _gelu(p[:, o + sgw:o + 2 * sgw])
    cos, sin = cos_ref[...], sin_ref[...]
    q_ref[...] = (q * cos + _swap_halves(q, ATTN_WIDTH) * sin).astype(BF16)
    kr_ref[...] = (k * cos[:, :KV_WIDTH] + _swap_halves(k, KV_WIDTH) * sin[:, :KV_WIDTH]).astype(BF16)
    k_ref[...] = k
    v_ref[...] = v
    su_ref[...] = su
    sv_ref[...] = _layer_norm(sv, lng_ref[...], lnb_ref[...]).astype(BF16)


def _inproj(x, mod3, row_fn, w_in, lng, lnb, cos_t, sin_t, rope_fn, tm):
    n, d = x.shape
    wn = w_in.shape[1]
    sgw = lng.shape[1]
    tok = lambda w: pl.BlockSpec((tm, w), lambda i: (i, 0))
    return pl.pallas_call(
        _inproj_kernel,
        grid=(n // tm,),
        in_specs=[tok(d),
                  pl.BlockSpec((1, 1, mod3.shape[2]), lambda i: (row_fn(i), 0, 0)),
                  pl.BlockSpec((d, wn), lambda i: (0, 0)),
                  pl.BlockSpec((1, sgw), lambda i: (0, 0)),
                  pl.BlockSpec((1, sgw), lambda i: (0, 0)),
                  pl.BlockSpec((tm, ATTN_WIDTH), lambda i: (rope_fn(i), 0)),
                  pl.BlockSpec((tm, ATTN_WIDTH), lambda i: (rope_fn(i), 0))],
        out_specs=[tok(ATTN_WIDTH), tok(KV_WIDTH), tok(KV_WIDTH), tok(KV_WIDTH), tok(sgw), tok(sgw)],
        out_shape=[jax.ShapeDtypeStruct((n, ATTN_WIDTH), BF16),
                   jax.ShapeDtypeStruct((n, KV_WIDTH), F32),
                   jax.ShapeDtypeStruct((n, KV_WIDTH), F32),
                   jax.ShapeDtypeStruct((n, KV_WIDTH), BF16),
                   jax.ShapeDtypeStruct((n, sgw), F32),
                   jax.ShapeDtypeStruct((n, sgw), BF16)],
        compiler_params=_cparams(("parallel",)),
        name="inproj",
    )(x, mod3, w_in, lng, lnb, cos_t, sin_t)


def _spatial_gate(su, svn, ws, bs):
    r = jnp.dot(ws, svn, preferred_element_type=F32)
    width = svn.shape[1]
    grp = lax.broadcasted_iota(jnp.int32, (SG_CHUNK, width), 1) // (width // SG_GROUPS)
    s = bs
    for g in range(SG_GROUPS):
        s = s + jnp.where(grp == g, r[g * SG_CHUNK:(g + 1) * SG_CHUNK, :], 0.0)
    return su * s


def _softmax_sink_pv(logit_parts, v_parts, sink):
    m = sink
    for lg in logit_parts:
        m = jnp.maximum(m, jnp.max(lg, axis=-1, keepdims=True))
    den = jnp.exp(sink - m)
    acc = None
    for lg, vv in zip(logit_parts, v_parts):
        p = jnp.exp(lg - m)
        den = den + jnp.sum(p, axis=-1, keepdims=True)
        pv = jnp.dot(p.astype(BF16), vv, preferred_element_type=F32)
        acc = pv if acc is None else acc + pv
    return acc / den


def _ctx_kernel(sink_ref, q_ref, k_ref, v_ref, su_ref, sv_ref, ws_ref, bs_ref, o_ref):
    s_len = q_ref.shape[0]
    scale = HEAD_DIM ** -0.5
    nt = (((1,), (1,)), ((), ()))
    for kv in range(N_KV_HEADS):
        kk = k_ref[:, kv * HEAD_DIM:(kv + 1) * HEAD_DIM].astype(BF16)
        vv = v_ref[:, kv * HEAD_DIM:(kv + 1) * HEAD_DIM].astype(BF16)
        for g in range(Q_PER_KV):
            h = kv * Q_PER_KV + g
            qh = q_ref[:, h * HEAD_DIM:(h + 1) * HEAD_DIM]
            lg = lax.dot_general(qh, kk, nt, preferred_element_type=F32) * scale
            oh = _softmax_sink_pv([lg], [vv], sink_ref[h])
            o_ref[:, h * HEAD_DIM:(h + 1) * HEAD_DIM] = oh.astype(BF16)
    for ci in range(s_len // SG_CHUNK):
        rows = slice(ci * SG_CHUNK, (ci + 1) * SG_CHUNK)
        sg = _spatial_gate(su_ref[rows, :], sv_ref[rows, :], ws_ref[...], bs_ref[...])
        o_ref[rows, ATTN_WIDTH:] = sg.astype(BF16)


def _ctx_attn(sink, q, k, v, su, sv, ws, bs, s_len):
    n = q.shape[0]
    sgw = su.shape[1]
    tok = lambda w: pl.BlockSpec((s_len, w), lambda b: (b, 0))
    return pl.pallas_call(
        _ctx_kernel,
        grid=(n // s_len,),
        in_specs=[pl.BlockSpec(memory_space=pltpu.SMEM),
                  tok(ATTN_WIDTH), tok(KV_WIDTH), tok(KV_WIDTH), tok(sgw), tok(sgw),
                  pl.BlockSpec(ws.shape, lambda b: (0, 0)),
                  pl.BlockSpec(bs.shape, lambda b: (0, 0))],
        out_specs=tok(ATTN_WIDTH + sgw),
        out_shape=jax.ShapeDtypeStruct((n, ATTN_WIDTH + sgw), BF16),
        compiler_params=_cparams(("parallel",)),
        name="ctx_attn",
    )(sink, q, k, v, su, sv, ws, bs)


def _lat_kernel(sink_ref, q_ref, k_ref, v_ref, ck_ref, cv_ref, su_ref, sv_ref, ws_ref, bs_ref, o_ref):
    t_len = k_ref.shape[0]
    band = 3 * BLOCK
    nb = pl.program_id(1)
    start = pl.multiple_of(jnp.clip((nb - 1) * BLOCK, 0, t_len - band), BLOCK)
    scale = HEAD_DIM ** -0.5
    nt = (((1,), (1,)), ((), ()))
    qpos = nb * BLOCK + lax.broadcasted_iota(jnp.int32, (BLOCK, band), 0)
    kpos = start + lax.broadcasted_iota(jnp.int32, (BLOCK, band), 1)
    keep = jnp.abs(qpos - kpos) <= WINDOW
    kwin = k_ref[pl.ds(start, band), :]
    vwin = v_ref[pl.ds(start, band), :].astype(BF16)
    ck = ck_ref[0, 0].astype(BF16)
    cv = cv_ref[0, 0].astype(BF16)
    for kv in range(N_KV_HEADS):
        cols = slice(kv * HEAD_DIM, (kv + 1) * HEAD_DIM)
        kk, vv, ckk, cvv = kwin[:, cols], vwin[:, cols], ck[:, cols], cv[:, cols]
        for g in range(Q_PER_KV):
            h = kv * Q_PER_KV + g
            qh = q_ref[:, h * HEAD_DIM:(h + 1) * HEAD_DIM]
            lb = lax.dot_general(qh, kk, nt, preferred_element_type=F32) * scale
            lb = jnp.where(keep, lb, NEG)
            lc = lax.dot_general(qh, ckk, nt, preferred_element_type=F32) * scale
            oh = _softmax_sink_pv([lb, lc], [vv, cvv], sink_ref[h])
            o_ref[:, h * HEAD_DIM:(h + 1) * HEAD_DIM] = oh.astype(BF16)
    sg = _spatial_gate(su_ref[...], sv_ref[...], ws_ref[...], bs_ref[...])
    o_ref[:, ATTN_WIDTH:] = sg.astype(BF16)


def _lat_attn(sink, q, kr, v, cache_k, cache_v, layer, su, sv, ws, bs, t_len):
    n = q.shape[0]
    sgw = su.shape[1]
    nblk = t_len // BLOCK
    past = cache_k.shape[2]
    blk = lambda w: pl.BlockSpec((BLOCK, w), lambda b, j: (b * nblk + j, 0))
    seq = lambda w: pl.BlockSpec((t_len, w), lambda b, j: (b, 0))
    cache = pl.BlockSpec((1, 1, past, KV_WIDTH), lambda b, j: (b, layer, 0, 0))
    return pl.pallas_call(
        _lat_kernel,
        grid=(n // t_len, nblk),
        in_specs=[pl.BlockSpec(memory_space=pltpu.SMEM),
                  blk(ATTN_WIDTH), seq(KV_WIDTH), seq(KV_WIDTH), cache, cache, blk(sgw), blk(sgw),
                  pl.BlockSpec(ws.shape, lambda b, j: (0, 0)),
                  pl.BlockSpec(bs.shape, lambda b, j: (0, 0))],
        out_specs=blk(ATTN_WIDTH + sgw),
        out_shape=jax.ShapeDtypeStruct((n, ATTN_WIDTH + sgw), BF16),
        compiler_params=_cparams(("parallel", "arbitrary")),
        name="lat_attn",
    )(sink, q, kr, v, cache_k, cache_v, su, sv, ws, bs)


def _outproj_kernel(alpha, cat_ref, x_ref, mod_ref, w_ref, g_ref, b_ref, x1_ref, h2_ref):
    d = x_ref.shape[1]
    m = mod_ref[0]
    gate1, shift2, scale2 = m[:, 2 * d:3 * d], m[:, 3 * d:4 * d], m[:, 4 * d:5 * d]
    y = jnp.dot(cat_ref[...], w_ref[...], preferred_element_type=F32)
    x1 = _layer_norm(alpha * x_ref[...] + gate1 * y, g_ref[...], b_ref[...])
    x1_ref[...] = x1
    h2_ref[...] = (x1 * (1.0 + scale2) + shift2).astype(BF16)


def _outproj(cat, x, mod3, row_fn, w_o, g, b, alpha, tm):
    n, d = x.shape
    tok = lambda w: pl.BlockSpec((tm, w), lambda i: (i, 0))
    return pl.pallas_call(
        functools.partial(_outproj_kernel, alpha),
        grid=(n // tm,),
        in_specs=[tok(d), tok(d),
                  pl.BlockSpec((1, 1, mod3.shape[2]), lambda i: (row_fn(i), 0, 0)),
                  pl.BlockSpec((d, d), lambda i: (0, 0)),
                  pl.BlockSpec((1, d), lambda i: (0, 0)),
                  pl.BlockSpec((1, d), lambda i: (0, 0))],
        out_specs=[tok(d), tok(d)],
        out_shape=[jax.ShapeDtypeStruct((n, d), F32), jax.ShapeDtypeStruct((n, d), BF16)],
        compiler_params=_cparams(("parallel",)),
        name="outproj",
    )(cat, x, mod3, w_o, g, b)


def _top_values(work, store_ref):
    for r in range(PEER_TOPK):
        m = jnp.max(work, axis=0, keepdims=True)
        store_ref[r:r + 1, :] = m
        if r + 1 < PEER_TOPK:
            work = jnp.where(work >= m, -jnp.inf, work)


def _peer_select_kernel(h2_ref, wq_ref, k1_ref, k2_ref, e1_ref, c_ref, e2_ref, r2_ref,
                        t1_ref, t2_ref, cand_ref, tau_ref):
    nt = (((1,), (1,)), ((), ()))
    qh = jnp.dot(h2_ref[...], wq_ref[...], preferred_element_type=F32)
    hi = lax.Precision.HIGHEST
    s1 = lax.dot_general(k1_ref[...], qh[:, :PEER_HALF], nt, precision=hi, preferred_element_type=F32)
    s2 = lax.dot_general(k2_ref[...], qh[:, PEER_HALF:], nt, precision=hi, preferred_element_type=F32)
    _top_values(s1, t1_ref)
    _top_values(s2, t2_ref)
    half = PEER_TOPK // 2
    t2_lo = t2_ref[0:half, :]
    for i in range(half):
        cand_ref[i * half:(i + 1) * half, :] = t1_ref[i:i + 1, :] + t2_lo
    cand_ref[half * half:half * half + half, :] = t1_ref[half:, :] + t2_ref[0:1, :]
    cand_ref[half * half + half:, :] = t1_ref[0:1, :] + t2_ref[half:, :]
    cand = cand_ref[...]
    _top_values(cand, tau_ref)
    tau = tau_ref[PEER_TOPK - 1:PEER_TOPK, :]
    top = t1_ref[0:1, :] + t2_ref[0:1, :]
    z = jnp.sum(jnp.where(cand >= tau, jnp.exp(cand - top), 0.0), axis=0, keepdims=True)
    cnt = jnp.zeros_like(s1)
    rank2 = jnp.zeros_like(s2)
    for j in range(PEER_TOPK):
        t2j = t2_ref[j:j + 1, :]
        cnt = cnt + jnp.where(s1 + t2j >= tau, 1.0, 0.0)
        rank2 = rank2 + jnp.where(t2j > s2, 1.0, 0.0)
    active1 = s1 >= t1_ref[PEER_TOPK - 1:PEER_TOPK, :]
    c_ref[0] = jnp.where(active1, cnt, 0.0)
    r2_ref[0] = rank2
    e1_ref[0] = jnp.exp(s1 - t1_ref[0:1, :])
    e2_ref[0] = jnp.exp(s2 - t2_ref[0:1, :]) / z


def _peer_select(h2, w_q, key1, key2, tm):
    n, d = h2.shape
    sel = jax.ShapeDtypeStruct((PEER_HEADS, PEER_KEYS, n), F32)
    out = pl.BlockSpec((1, PEER_KEYS, tm), lambda i, h: (h, 0, i))
    ncand = (PEER_TOPK // 2) ** 2 + PEER_TOPK
    return pl.pallas_call(
        _peer_select_kernel,
        grid=(n // tm, PEER_HEADS),
        in_specs=[pl.BlockSpec((tm, d), lambda i, h: (i, 0)),
                  pl.BlockSpec((d, PEER_QDIM), lambda i, h: (0, h)),
                  pl.BlockSpec(key1.shape, lambda i, h: (0, 0)),
                  pl.BlockSpec(key2.shape, lambda i, h: (0, 0))],
        out_specs=[out, out, out, out],
        out_shape=[sel, sel, sel, sel],
        scratch_shapes=[pltpu.VMEM((PEER_TOPK, tm), F32), pltpu.VMEM((PEER_TOPK, tm), F32),
                        pltpu.VMEM((ncand, tm), F32), pltpu.VMEM((PEER_TOPK, tm), F32)],
        compiler_params=_cparams(("parallel", "arbitrary")),
        name="peer_select",
    )(h2, w_q, key1, key2)


def _table_prep_kernel(u_ref, v_ref, ub_ref, vt_ref):
    ub_ref[0] = u_ref[0].astype(BF16)
    vt_ref[0] = v_ref[0].T.astype(BF16)


def _table_prep(peer_u, peer_v, te):
    depth, ne, d = peer_u.shape
    return pl.pallas_call(
        _table_prep_kernel,
        grid=(depth, ne // te),
        in_specs=[pl.BlockSpec((1, te, d), lambda l, j: (l, j, 0)),
                  pl.BlockSpec((1, te, d), lambda l, j: (l, j, 0))],
        out_specs=[pl.BlockSpec((1, te, d), lambda l, j: (l, j, 0)),
                   pl.BlockSpec((1, d, te), lambda l, j: (l, 0, j))],
        out_shape=[jax.ShapeDtypeStruct((depth, ne, d), BF16),
                   jax.ShapeDtypeStruct((depth, d, ne), BF16)],
        compiler_params=_cparams(("parallel", "parallel")),
        name="table_prep",
    )(peer_u, peer_v)


def _peer_experts_kernel(alpha, h2_ref, u_ref, vt_ref, e1_ref, c_ref, e2_ref, r2_ref,
                         x1_ref, mod_ref, g_ref, b_ref, o_ref, acc_ref, wt_ref):
    e = pl.program_id(1)
    te = u_ref.shape[1]
    n_i1 = te // PEER_KEYS
    nt = (((1,), (1,)), ((), ()))

    @pl.when(e == 0)
    def _():
        acc_ref[...] = jnp.zeros_like(acc_ref)

    at = lax.dot_general(u_ref[0], h2_ref[...], nt, preferred_element_type=F32)
    for j in range(n_i1):
        i1 = e * n_i1 + j
        gate = None
        for h in range(PEER_HEADS):
            picked = r2_ref[h] < c_ref[h, pl.ds(i1, 1), :]
            term = jnp.where(picked, e2_ref[h], 0.0) * e1_ref[h, pl.ds(i1, 1), :]
            gate = term if gate is None else gate + term
        rows = slice(j * PEER_KEYS, (j + 1) * PEER_KEYS)
        wt_ref[rows, :] = (_gelu(at[rows, :]) * gate).astype(BF16)
    acc_ref[...] += jnp.dot(vt_ref[0], wt_ref[...], preferred_element_type=F32)

    @pl.when(e == pl.num_programs(1) - 1)
    def _():
        d = x1_ref.shape[1]
        gate2 = mod_ref[0][:, 5 * d:6 * d]
        y = acc_ref[...].T
        o_ref[...] = _layer_norm(alpha * x1_ref[...] + gate2 * y, g_ref[...], b_ref[...])


def _peer_experts(h2, ub, vt, layer, sel, x1, mod3, row_fn, g, b, alpha, tt, te):
    n, d = h2.shape
    ne = ub.shape[1]
    e1, cnt, e2, r2 = sel
    selspec = pl.BlockSpec((PEER_HEADS, PEER_KEYS, tt), lambda i, e: (0, 0, i))
    tok = lambda w: pl.BlockSpec((tt, w), lambda i, e: (i, 0))
    return pl.pallas_call(
        functools.partial(_peer_experts_kernel, alpha),
        grid=(n // tt, ne // te),
        in_specs=[tok(d),
                  pl.BlockSpec((1, te, d), lambda i, e: (layer, e, 0)),
                  pl.BlockSpec((1, d, te), lambda i, e: (layer, 0, e)),
                  selspec, selspec, selspec, selspec,
                  tok(d),
                  pl.BlockSpec((1, 1, mod3.shape[2]), lambda i, e: (row_fn(i), 0, 0)),
                  pl.BlockSpec((1, d), lambda i, e: (0, 0)),
                  pl.BlockSpec((1, d), lambda i, e: (0, 0))],
        out_specs=tok(d),
        out_shape=jax.ShapeDtypeStruct((n, d), F32),
        scratch_shapes=[pltpu.VMEM((d, tt), F32), pltpu.VMEM((te, tt), BF16)],
        compiler_params=_cparams(("parallel", "arbitrary")),
        name="peer_experts",
    )(h2, ub, vt, e1, cnt, e2, r2, x1, mod3, g, b)


def _rope_tables(t_len, tm):
    rows = t_len // GRID_W
    row = jnp.repeat(jnp.arange(rows), GRID_W).astype(F32)
    col = jnp.tile(jnp.arange(GRID_W), rows).astype(F32)
    inv_freq = 1.0 / (ROPE_BASE ** (jnp.arange(0, ROT_AXIS, 2, dtype=F32) / ROT_AXIS))
    ang_r = row[:, None] * inv_freq[None, :]
    ang_c = col[:, None] * inv_freq[None, :]
    cos_h = jnp.concatenate([jnp.cos(ang_r), jnp.cos(ang_r), jnp.cos(ang_c), jnp.cos(ang_c)], axis=-1)
    sin_h = jnp.concatenate([-jnp.sin(ang_r), jnp.sin(ang_r), -jnp.sin(ang_c), jnp.sin(ang_c)], axis=-1)
    cos_t = jnp.concatenate([jnp.ones((tm, ATTN_WIDTH), F32), jnp.tile(cos_h, (1, N_Q_HEADS))], axis=0)
    sin_t = jnp.concatenate([jnp.zeros((tm, ATTN_WIDTH), F32), jnp.tile(sin_h, (1, N_Q_HEADS))], axis=0)
    return cos_t, sin_t


def kernel(x_prompt, x_sample, cache_k, cache_v, c, c_ctx, w_mod, b_mod, w_in, sg_ln_g, sg_ln_b, w_s, b_s,
           attn_sink, w_o, ln1_g, ln1_b, peer_wq, peer_k1, peer_k2, peer_u, peer_v, ln2_g, ln2_b):
    batch, s_len, d = x_prompt.shape
    dec_batch, t_len, _ = x_sample.shape
    depth = w_mod.shape[0]
    past = cache_k.shape[2]
    sgw = sg_ln_g.shape[1]
    alpha = (2.0 * depth) ** 0.25
    assert dec_batch + 1 <= MOD_ROWS and t_len >= 3 * BLOCK and t_len % GRID_W == 0
    tm = 256
    tt = 512
    te = 1024
    assert s_len % tm == 0 and t_len % tt == 0 and (batch * s_len) % tt == 0

    cond = jnp.zeros((MOD_ROWS, d), F32).at[0].set(c_ctx).at[1:1 + dec_batch].set(c)
    mod = _modulation(cond, w_mod, b_mod)
    cos_t, sin_t = _rope_tables(t_len, tm)
    ub, vt = _table_prep(peer_u, peer_v, te)
    ck = cache_k.reshape(dec_batch, depth, past, KV_WIDTH)
    cv = cache_v.reshape(dec_batch, depth, past, KV_WIDTH)

    xp = x_prompt.reshape(batch * s_len, d)
    xs = x_sample.reshape(dec_batch * t_len, d)
    new_k, new_v = [], []
    for l in range(depth):
        mod3 = mod[l].reshape(MOD_ROWS, 1, 6 * d)
        w_in_b = w_in[l].astype(BF16)
        w_o_b = w_o[l].astype(BF16)
        w_q_b = peer_wq[l].astype(BF16)
        ws = w_s[l].reshape(SG_GROUPS * SG_CHUNK, SG_CHUNK).astype(BF16)
        bs = jnp.repeat(b_s[l].T, sgw // SG_GROUPS, axis=1)
        lng, lnb = sg_ln_g[l][None], sg_ln_b[l][None]
        sink = attn_sink[l]
        outs = []
        for latent, x in ((False, xp), (True, xs)):
            if latent:
                row_fn = lambda i, tile=tm: 1 + i // (t_len // tile)
                rope_fn = lambda i: 1 + i % (t_len // tm)
            else:
                row_fn = lambda i, tile=tm: 0 * i
                rope_fn = lambda i: 0 * i
            q, k, v, kr, su, sv = _inproj(x, mod3, row_fn, w_in_b, lng, lnb, cos_t, sin_t, rope_fn, tm)
            if latent:
                cat = _lat_attn(sink, q, kr, v, ck, cv, l, su, sv, ws, bs, t_len)
            else:
                cat = _ctx_attn(sink, q, k, v, su, sv, ws, bs, s_len)
                new_k.append(k.reshape(batch, s_len, N_KV_HEADS, HEAD_DIM))
                new_v.append(v.reshape(batch, s_len, N_KV_HEADS, HEAD_DIM))
            x1, h2 = _outproj(cat, x, mod3, row_fn, w_o_b, ln1_g[l][None], ln1_b[l][None], alpha, tm)
            sel = _peer_select(h2, w_q_b, peer_k1[l], peer_k2[l], tm)
            row_fn_t = functools.partial(row_fn, tile=tt)
            outs.append(_peer_experts(h2, ub, vt, l, sel, x1, mod3, row_fn_t,
                                      ln2_g[l][None], ln2_b[l][None], alpha, tt, te))
        xp, xs = outs
    state_k = jnp.stack(new_k, axis=1)
    state_v = jnp.stack(new_v, axis=1)
    return (xp.reshape(batch, s_len, d), xs.reshape(dec_batch, t_len, d), state_k, state_v)
```
